```python
import math
import jax, jax.numpy as jnp
from jax import lax
import numpy as np

D_MODEL = 1024
BATCH = 8
SEQ = 2048
DEPTH = 4
DEC_BATCH = 128
DEC_SEQ = 8
PAST_LEN = 2048
PAGE_SIZE = 128

HEAD_DIM = 64
N_DIFF = D_MODEL // (2 * HEAD_DIM)
DQK_DIFF = HEAD_DIM // 2
DV_DIFF = HEAD_DIM
N_FOX = D_MODEL // (2 * HEAD_DIM)
HD_FOX = HEAD_DIM
DIFF_W = N_DIFF * DV_DIFF
FOX_W = N_FOX * HD_FOX
MIX_W = DIFF_W + FOX_W
DIFF_QK_W = N_DIFF * 2 * DQK_DIFF
O_DQ = 0
O_DK = O_DQ + DIFF_QK_W
O_DV = O_DK + DIFF_QK_W
O_FQ = O_DV + DIFF_W
O_FK = O_FQ + FOX_W
O_FV = O_FK + FOX_W
O_FF = O_FV + FOX_W
N_IN = O_FF + N_FOX
D_FF = ((8 * D_MODEL // 3 + 127) // 128) * 128
N_BUCKETS = 32
MAX_DISTANCE = 128
Q_BLOCK = 128
EPS = 1e-6
NEG = -1e30

kernel_name = "hymba_diff_fox_macaron_step"


def rms_norm(x, g):
    xf = x.astype(jnp.float32)
    y = xf * lax.rsqrt(jnp.mean(xf * xf, axis=-1, keepdims=True) + EPS)
    return (y * g.astype(jnp.float32)).astype(x.dtype)


def swiglu(x, w_in, w_out):
    gate, up = jnp.split(x @ w_in, 2, axis=-1)
    return (jax.nn.silu(gate) * up) @ w_out


def t5_bucket(rel):
    n = jnp.maximum(rel, 0)
    max_exact = N_BUCKETS // 2
    nf = jnp.maximum(n, 1).astype(jnp.float32)
    large = max_exact + (jnp.log(nf / max_exact) / math.log(MAX_DISTANCE / max_exact)
                         * (N_BUCKETS - max_exact)).astype(jnp.int32)
    large = jnp.minimum(large, N_BUCKETS - 1)
    return jnp.where(n < max_exact, n, large)


def diff_block(q, k, v, rel_bias, mask, lam):
    b, tk = k.shape[:2]
    k = k.reshape(b, tk, N_DIFF, 2, DQK_DIFF)
    s = jnp.einsum('bqhcd,bkhcd->bchqk', q, k).astype(jnp.float32) * (DQK_DIFF ** -0.5) + rel_bias
    a = jax.nn.softmax(jnp.where(mask, s, NEG), axis=-1)
    w = a[:, 0] - lam * a[:, 1]
    return jnp.einsum('bhqk,bkhd->bqhd', w.astype(v.dtype), v)


def fox_block(q, k, v, cq, ck, mask):
    decay = jnp.transpose(cq, (0, 2, 1))[..., :, None] - jnp.transpose(ck, (0, 2, 1))[..., None, :]
    s = jnp.einsum('bqhd,bkhd->bhqk', q, k).astype(jnp.float32) * (HD_FOX ** -0.5) + decay
    a = jax.nn.softmax(jnp.where(mask, s, NEG), axis=-1)
    return jnp.einsum('bhqk,bkhd->bqhd', a.astype(v.dtype), v)


def gather_pages(cache_l, page_table):
    pages = cache_l[page_table]
    return pages.reshape((page_table.shape[0], -1) + cache_l.shape[2:])


def token_mixing(a, l, past, w_in, b_forget, lambda_qk, diff_norm_g, fox_norm_g, w_out, t5_table):
    bsz, t, _ = a.shape
    p = a @ w_in[l]
    dq = p[..., O_DQ:O_DK].reshape(bsz, t, N_DIFF, 2, DQK_DIFF)
    dk = p[..., O_DK:O_DV].reshape(bsz, t, N_DIFF, 2 * DQK_DIFF)
    dv = p[..., O_DV:O_FQ].reshape(bsz, t, N_DIFF, DV_DIFF)
    fq = p[..., O_FQ:O_FK].reshape(bsz, t, N_FOX, HD_FOX)
    fk = p[..., O_FK:O_FV].reshape(bsz, t, N_FOX, HD_FOX)
    fv = p[..., O_FV:O_FF].reshape(bsz, t, N_FOX, HD_FOX)
    logf = jax.nn.log_sigmoid(p[..., O_FF:].astype(jnp.float32) + b_forget[l].astype(jnp.float32))
    new_rows = (dk, dv, fk, fv, logf.astype(a.dtype))
    if past is None:
        q_off = 0
        k_d, v_d, k_f, v_f, lf_all = dk, dv, fk, fv, logf
    else:
        pdk, pdv, pfk, pfv, plf = past
        q_off = pdk.shape[1]
        k_d = jnp.concatenate([pdk, dk], axis=1)
        v_d = jnp.concatenate([pdv, dv], axis=1)
        k_f = jnp.concatenate([pfk, fk], axis=1)
        v_f = jnp.concatenate([pfv, fv], axis=1)
        lf_all = jnp.concatenate([plf.astype(jnp.float32), logf], axis=1)
    c_all = jnp.cumsum(lf_all.astype(jnp.float32), axis=1)
    cq_all = c_all[:, q_off:]
    lam_init = 0.8 - 0.6 * math.exp(-0.3 * l)
    lq = lambda_qk[l].astype(jnp.float32)
    lam = jnp.exp(jnp.sum(lq[0] * lq[1])) - jnp.exp(jnp.sum(lq[2] * lq[3])) + lam_init
    outs_d, outs_f = [], []
    for start in range(0, t, Q_BLOCK):
        end = min(start + Q_BLOCK, t)
        kend = q_off + end
        qpos = q_off + jnp.arange(start, end, dtype=jnp.int32)
        kpos = jnp.arange(kend, dtype=jnp.int32)
        rel = qpos[:, None] - kpos[None, :]
        mask = rel >= 0
        rel_bias = jnp.transpose(t5_table[t5_bucket(rel)].astype(jnp.float32), (2, 0, 1))
        outs_d.append(diff_block(dq[:, start:end], k_d[:, :kend], v_d[:, :kend], rel_bias, mask, lam))
        outs_f.append(fox_block(fq[:, start:end], k_f[:, :kend], v_f[:, :kend],
                                cq_all[:, start:end], c_all[:, :kend], mask))
    od = rms_norm(jnp.concatenate(outs_d, axis=1), diff_norm_g[l]) * (1.0 - lam_init)
    of = rms_norm(jnp.concatenate(outs_f, axis=1), fox_norm_g[l])
    mix = jnp.concatenate([od.reshape(bsz, t, DIFF_W), of.reshape(bsz, t, FOX_W)], axis=-1)
    return mix @ w_out[l], new_rows


def decoder_layer(x, l, past, norm_g, w_ffn_in, w_ffn_out, w_in, b_forget, lambda_qk,
                  diff_norm_g, fox_norm_g, w_out, t5_table):
    g = norm_g[l]
    h = x + 0.5 * rms_norm(swiglu(rms_norm(x, g[0]), w_ffn_in[l, 0], w_ffn_out[l, 0]), g[1])
    m, rows = token_mixing(rms_norm(h, g[2]), l, past, w_in, b_forget, lambda_qk,
                           diff_norm_g, fox_norm_g, w_out, t5_table)
    h = h + rms_norm(m, g[3])
    h = h + 0.5 * rms_norm(swiglu(rms_norm(h, g[4]), w_ffn_in[l, 1], w_ffn_out[l, 1]), g[5])
    return h, rows


def setup_inputs(seed: int = 0) -> dict:
    key = jax.random.key(seed)
    ks = jax.random.split(key, 20)
    f32 = jnp.float32
    n_pages = PAST_LEN // PAGE_SIZE
    n_used = DEC_BATCH * n_pages
    n_pool = n_used + max(1, n_used // 4)
    page_table = jax.random.permutation(ks[0], n_pool)[:n_used].reshape(DEC_BATCH, n_pages).astype(jnp.int32)
    return {
        "x_prompt": jax.random.normal(ks[1], (BATCH, SEQ, D_MODEL), f32),
        "x_sample": jax.random.normal(ks[2], (DEC_BATCH, DEC_SEQ, D_MODEL), f32),
        "cache_diff_k": jax.random.normal(ks[3], (DEPTH, n_pool, PAGE_SIZE, N_DIFF, 2 * DQK_DIFF), f32),
        "cache_diff_v": jax.random.normal(ks[4], (DEPTH, n_pool, PAGE_SIZE, N_DIFF, DV_DIFF), f32),
        "cache_fox_k": jax.random.normal(ks[5], (DEPTH, n_pool, PAGE_SIZE, N_FOX, HD_FOX), f32),
        "cache_fox_v": jax.random.normal(ks[6], (DEPTH, n_pool, PAGE_SIZE, N_FOX, HD_FOX), f32),
        "cache_fox_logf": jax.nn.log_sigmoid(4.5 + jax.random.normal(ks[7], (DEPTH, n_pool, PAGE_SIZE, N_FOX), f32)),
        "page_table": page_table,
        "norm_g": 1.0 + 0.1 * jax.random.normal(ks[8], (DEPTH, 6, D_MODEL), f32),
        "w_ffn_in": jax.random.normal(ks[9], (DEPTH, 2, D_MODEL, 2 * D_FF), f32) * D_MODEL ** -0.5,
        "w_ffn_out": jax.random.normal(ks[10], (DEPTH, 2, D_FF, D_MODEL), f32) * D_FF ** -0.5,
        "w_in": jax.random.normal(ks[11], (DEPTH, D_MODEL, N_IN), f32) * D_MODEL ** -0.5,
        "b_forget": 3.0 + 3.0 * jax.random.uniform(ks[12], (DEPTH, N_FOX), f32),
        "lambda_qk": 0.1 * jax.random.normal(ks[13], (DEPTH, 4, DQK_DIFF), f32),
        "diff_norm_g": 1.0 + 0.1 * jax.random.normal(ks[14], (DEPTH, DV_DIFF), f32),
        "fox_norm_g": 1.0 + 0.1 * jax.random.normal(ks[15], (DEPTH, HD_FOX), f32),
        "w_out": jax.random.normal(ks[16], (DEPTH, MIX_W, D_MODEL), f32) * MIX_W ** -0.5,
        "t5_table": 0.5 * jax.random.normal(ks[17], (N_BUCKETS, N_DIFF), f32),
    }


def reference(x_prompt, x_sample, cache_diff_k, cache_diff_v, cache_fox_k, cache_fox_v, cache_fox_logf,
              page_table, norm_g, w_ffn_in, w_ffn_out, w_in, b_forget, lambda_qk, diff_norm_g,
              fox_norm_g, w_out, t5_table):
    hp, hs = x_prompt, x_sample
    rows_p, rows_s = [], []
    for l in range(DEPTH):
        hp, rp = decoder_layer(hp, l, None, norm_g, w_ffn_in, w_ffn_out, w_in, b_forget, lambda_qk,
                               diff_norm_g, fox_norm_g, w_out, t5_table)
        past = (gather_pages(cache_diff_k[l], page_table), gather_pages(cache_diff_v[l], page_table),
                gather_pages(cache_fox_k[l], page_table), gather_pages(cache_fox_v[l], page_table),
                gather_pages(cache_fox_logf[l], page_table))
        hs, rs = decoder_layer(hs, l, past, norm_g, w_ffn_in, w_ffn_out, w_in, b_forget, lambda_qk,
                               diff_norm_g, fox_norm_g, w_out, t5_table)
        rows_p.append(rp)
        rows_s.append(rs)

    def stack(rows, i):
        return jnp.stack([r[i] for r in rows])

    return (hp, hs,
            stack(rows_p, 0), stack(rows_p, 1), stack(rows_p, 2), stack(rows_p, 3), stack(rows_p, 4),
            stack(rows_s, 0), stack(rows_s, 1), stack(rows_s, 2), stack(rows_s, 3), stack(rows_s, 4))
```

```python
import functools
import math

import jax
import jax.numpy as jnp
from jax import lax
from jax.experimental import pallas as pl
from jax.experimental.pallas import tpu as pltpu

F32 = jnp.float32
BF16 = jnp.bfloat16
EPS = 1e-6
NEG = -1e30
HEAD = 64
DQK = 32
NH = 8
MIXW = NH * HEAD
N_BUCKETS = 32
MAX_DISTANCE = 128
LANES = 128
VMEM_LIMIT = 56 * 1024 * 1024


def _cparams(sem):
    return pltpu.CompilerParams(dimension_semantics=sem, vmem_limit_bytes=VMEM_LIMIT)


def _rms(x, g):
    ms = jnp.mean(x * x, axis=-1, keepdims=True)
    return x * lax.rsqrt(ms + EPS) * g


def _dot(a, b):
    return jnp.dot(a, b, preferred_element_type=F32)


def _dot_nt(a, b):
    return lax.dot_general(a, b, (((1,), (1,)), ((), ())), preferred_element_type=F32)


def _split3(x):
    hi = x.astype(BF16)
    r1 = x - hi.astype(F32)
    mid = r1.astype(BF16)
    lo = (r1 - mid.astype(F32)).astype(BF16)
    return hi, mid, lo


def _log_sigmoid(x):
    return jnp.minimum(x, 0.0) - jnp.log1p(jnp.exp(-jnp.abs(x)))


def _const_spec(block, index):
    return pl.BlockSpec(block, index, pipeline_mode=pl.Buffered(1))


def _swiglu_tile(a_ref, h_ref, wi_ref, wo_ref, d_ff, chunk):
    for c0 in range(0, d_ff, chunk):
        a = a_ref[...]
        gate = _dot(a, wi_ref[:, c0:c0 + chunk])
        up = _dot(a, wi_ref[:, d_ff + c0:d_ff + c0 + chunk])
        h_ref[:, c0:c0 + chunk] = (gate * jax.nn.sigmoid(gate) * up).astype(BF16)
    return _dot(h_ref[...], wo_ref[...])


def _ffn_a_body(x_ref, g_ref, wi_ref, wo_ref, o_ref, a_ref, h_ref, *, d_ff, chunk):
    x = x_ref[...]
    a_ref[...] = _rms(x, g_ref[0:1, :]).astype(BF16)
    y = _swiglu_tile(a_ref, h_ref, wi_ref, wo_ref, d_ff, chunk)
    o_ref[...] = x + 0.5 * _rms(y, g_ref[1:2, :])


def _ffn_a(x, norm_g, wi, wo, l, tm):
    n, d = x.shape
    d_ff = wo.shape[2]
    chunk = 256 if d_ff % 256 == 0 else LANES
    return pl.pallas_call(
        functools.partial(_ffn_a_body, d_ff=d_ff, chunk=chunk),
        grid=(n // tm,),
        in_specs=[
            pl.BlockSpec((tm, d), lambda i: (i, 0)),
            _const_spec((None, 6, d), lambda i: (l, 0, 0)),
            _const_spec((None, None, d, 2 * d_ff), lambda i: (l, 0, 0, 0)),
            _const_spec((None, None, d_ff, d), lambda i: (l, 0, 0, 0)),
        ],
        out_specs=pl.BlockSpec((tm, d), lambda i: (i, 0)),
        out_shape=jax.ShapeDtypeStruct((n, d), F32),
        scratch_shapes=[pltpu.VMEM((tm, d), BF16), pltpu.VMEM((tm, d_ff), BF16)],
        compiler_params=_cparams(("parallel",)),
        name="ffn_a",
    )(x, norm_g, wi, wo)


def _ffn_b_body(x_ref, mix_ref, g_ref, wout_ref, wi_ref, wo_ref, o_ref, a_ref, h_ref, *, d_ff, chunk):
    m = _dot(mix_ref[...].astype(BF16), wout_ref[...])
    h2 = x_ref[...] + _rms(m, g_ref[3:4, :])
    a_ref[...] = _rms(h2, g_ref[4:5, :]).astype(BF16)
    y = _swiglu_tile(a_ref, h_ref, wi_ref, wo_ref, d_ff, chunk)
    o_ref[...] = h2 + 0.5 * _rms(y, g_ref[5:6, :])


def _ffn_b(x, mix, norm_g, w_out, wi, wo, l, tm):
    n, d = x.shape
    d_ff = wo.shape[2]
    chunk = 256 if d_ff % 256 == 0 else LANES
    return pl.pallas_call(
        functools.partial(_ffn_b_body, d_ff=d_ff, chunk=chunk),
        grid=(n // tm,),
        in_specs=[
            pl.BlockSpec((tm, d), lambda i: (i, 0)),
            pl.BlockSpec((tm, mix.shape[1]), lambda i: (i, 0)),
            _const_spec((None, 6, d), lambda i: (l, 0, 0)),
            _const_spec((None, mix.shape[1], d), lambda i: (l, 0, 0)),
            _const_spec((None, None, d, 2 * d_ff), lambda i: (l, 1, 0, 0)),
            _const_spec((None, None, d_ff, d), lambda i: (l, 1, 0, 0)),
        ],
        out_specs=pl.BlockSpec((tm, d), lambda i: (i, 0)),
        out_shape=jax.ShapeDtypeStruct((n, d), F32),
        scratch_shapes=[pltpu.VMEM((tm, d), BF16), pltpu.VMEM((tm, d_ff), BF16)],
        compiler_params=_cparams(("parallel",)),
        name="ffn_b",
    )(x, mix, norm_g, w_out, wi, wo)


def _proj_p_body(*refs, tm, tk, aliased):
    if aliased:
        refs = refs[5:]
    (x_ref, g_ref, wq_ref, wkvt_ref, wft_ref, bf_ref,
     dk_ref, dv_ref, fk_ref, fv_ref, lf_ref, q_ref, kt_ref, vt_ref, ck_ref, carry_ref) = refs
    ti = pl.program_id(1)
    a = _rms(x_ref[...], g_ref[2:3, :]).astype(BF16)

    q = _dot(a, wq_ref[...])
    lane = lax.broadcasted_iota(jnp.int32, (1, q.shape[1]), 1)
    q = q * jnp.where(lane < MIXW, DQK ** -0.5, HEAD ** -0.5)
    q_ref[...] = q.astype(BF16)

    nsub = tm // tk
    for s, (row_ref, dst_ref, off) in enumerate(
            ((dk_ref, kt_ref, 0), (dv_ref, vt_ref, 0), (fk_ref, kt_ref, MIXW), (fv_ref, vt_ref, MIXW))):
        sec = _dot_nt(wkvt_ref[s * MIXW:(s + 1) * MIXW, :], a)
        row_ref[...] = sec
        secb = sec.astype(BF16)
        for j in range(nsub):
            dst_ref[j, off:off + MIXW, :] = secb[:, j * tk:(j + 1) * tk]

    pf = _dot_nt(wft_ref[...], a) + bf_ref[...]
    lf = _log_sigmoid(pf)
    lf_ref[...] = lf[:NH, :]

    @pl.when(ti == 0)
    def _():
        carry_ref[...] = jnp.zeros_like(carry_ref)

    r = lax.broadcasted_iota(jnp.int32, (tm, tm), 0)
    c = lax.broadcasted_iota(jnp.int32, (tm, tm), 1)
    tri = jnp.where(r <= c, 1.0, 0.0).astype(BF16)
    hi, mid, lo = _split3(lf)
    cs = _dot(hi, tri) + _dot(mid, tri) + _dot(lo, tri) + carry_ref[...]
    carry_ref[...] = cs[:, tm - 1:tm]
    for j in range(nsub):
        ck_ref[j] = cs[:NH, j * tk:(j + 1) * tk]


def _proj_p(h, row_off_blocks, prev_rows, norm_g, wq, wkvt, wft, bfcol, l, depth, b, t, tm, tk):
    d = h.shape[1]
    nt = t // tm
    nsub = tm // tk
    nk = t // tk
    aliased = prev_rows is not None
    row_shape = jax.ShapeDtypeStruct((depth, b, MIXW, t), F32)
    out_shape = [row_shape] * 4 + [
        jax.ShapeDtypeStruct((depth, b, NH, t), F32),
        jax.ShapeDtypeStruct((b * t, 2 * MIXW), BF16),
        jax.ShapeDtypeStruct((b, nk, 2 * MIXW, tk), BF16),
        jax.ShapeDtypeStruct((b, nk, 2 * MIXW, tk), BF16),
        jax.ShapeDtypeStruct((b, nk, NH, tk), F32),
    ]
    row_spec = pl.BlockSpec((None, None, MIXW, tm), lambda bi, ti: (l, bi, 0, ti))
    out_specs = [row_spec] * 4 + [
        pl.BlockSpec((None, None, NH, tm), lambda bi, ti: (l, bi, 0, ti)),
        pl.BlockSpec((tm, 2 * MIXW), lambda bi, ti: (bi * nt + ti, 0)),
        pl.BlockSpec((None, nsub, 2 * MIXW, tk), lambda bi, ti: (bi, ti, 0, 0)),
        pl.BlockSpec((None, nsub, 2 * MIXW, tk), lambda bi, ti: (bi, ti, 0, 0)),
        pl.BlockSpec((None, nsub, NH, tk), lambda bi, ti: (bi, ti, 0, 0)),
    ]
    in_specs = [
        pl.BlockSpec((tm, d), lambda bi, ti: (row_off_blocks + bi * nt + ti, 0)),
        _const_spec((None, 6, d), lambda bi, ti: (l, 0, 0)),
        _const_spec((None, d, 2 * MIXW), lambda bi, ti: (l, 0, 0)),
        _const_spec((None, 4 * MIXW, d), lambda bi, ti: (l, 0, 0)),
        _const_spec((None, 16, d), lambda bi, ti: (l, 0, 0)),
        _const_spec((None, 16, 1), lambda bi, ti: (l, 0, 0)),
    ]
    args = [h, norm_g, wq, wkvt, wft, bfcol]
    aliases = {}
    if aliased:
        in_specs = [pl.BlockSpec(memory_space=pl.ANY)] * 5 + in_specs
        args = list(prev_rows) + args
        aliases = {i: i for i in range(5)}
    return pl.pallas_call(
        functools.partial(_proj_p_body, tm=tm, tk=tk, aliased=aliased),
        grid=(b, nt),
        in_specs=in_specs,
        out_specs=out_specs,
        out_shape=out_shape,
        scratch_shapes=[pltpu.VMEM((16, 1), F32)],
        input_output_aliases=aliases,
        compiler_params=_cparams(("parallel", "arbitrary")),
        name="proj_prompt",
    )(*args)


def _proj_s_body(*refs, aliased):
    if aliased:
        refs = refs[5:]
    (x_ref, g_ref, wq_ref, wkv_ref, wf_ref, bf_ref,
     dk_ref, dv_ref, fk_ref, fv_ref, lf_ref, q_ref) = refs
    a = _rms(x_ref[...], g_ref[2:3, :]).astype(BF16)
    q = _dot(a, wq_ref[...])
    lane = lax.broadcasted_iota(jnp.int32, (1, q.shape[1]), 1)
    q_ref[...] = q * jnp.where(lane < MIXW, DQK ** -0.5, HEAD ** -0.5)
    for s, row_ref in enumerate((dk_ref, dv_ref, fk_ref, fv_ref)):
        row_ref[...] = _dot(a, wkv_ref[:, s * MIXW:(s + 1) * MIXW])
    pf = _dot(a, wf_ref[...]) + bf_ref[...]
    lf_ref[...] = _log_sigmoid(pf)[:, :NH]


def _proj_s(h, row_off_blocks, prev_rows, norm_g, wq, wkv, wf, bfrow, l, depth, ns, tm):
    d = h.shape[1]
    aliased = prev_rows is not None
    row_shape = jax.ShapeDtypeStruct((depth, ns, MIXW), F32)
    out_shape = [row_shape] * 4 + [
        jax.ShapeDtypeStruct((depth, ns, NH), F32),
        jax.ShapeDtypeStruct((ns, 2 * MIXW), F32),
    ]
    row_spec = pl.BlockSpec((None, tm, MIXW), lambda i: (l, i, 0))
    out_specs = [row_spec] * 4 + [
        pl.BlockSpec((None, tm, NH), lambda i: (l, i, 0)),
        pl.BlockSpec((tm, 2 * MIXW), lambda i: (i, 0)),
    ]
    in_specs = [
        pl.BlockSpec((tm, d), lambda i: (row_off_blocks + i, 0)),
        _const_spec((None, 6, d), lambda i: (l, 0, 0)),
        _const_spec((None, d, 2 * MIXW), lambda i: (l, 0, 0)),
        _const_spec((None, d, 4 * MIXW), lambda i: (l, 0, 0)),
        _const_spec((None, d, LANES), lambda i: (l, 0, 0)),
        _const_spec((None, 1, LANES), lambda i: (l, 0, 0)),
    ]
    args = [h, norm_g, wq, wkv, wf, bfrow]
    aliases = {}
    if aliased:
        in_specs = [pl.BlockSpec(memory_space=pl.ANY)] * 5 + in_specs
        args = list(prev_rows) + args
        aliases = {i: i for i in range(5)}
    return pl.pallas_call(
        functools.partial(_proj_s_body, aliased=aliased),
        grid=(ns // tm,),
        in_specs=in_specs,
        out_specs=out_specs,
        out_shape=out_shape,
        input_output_aliases=aliases,
        compiler_params=_cparams(("parallel",)),
        name="proj_sample",
    )(*args)


def _t5_bucket(rel):
    n = jnp.maximum(rel, 0)
    max_exact = N_BUCKETS // 2
    nf = jnp.maximum(n, 1).astype(F32)
    large = max_exact + (jnp.log(nf / max_exact) / math.log(MAX_DISTANCE / max_exact)
                         * (N_BUCKETS - max_exact)).astype(jnp.int32)
    large = jnp.minimum(large, N_BUCKETS - 1)
    return jnp.where(n < max_exact, n, large)


def _t5_gather(table_ref, rel, h):
    bucket = _t5_bucket(rel)
    acc = jnp.zeros(rel.shape, F32)
    for bkt in range(N_BUCKETS):
        acc = jnp.where(bucket == bkt, table_ref[bkt, h], acc)
    return acc - table_ref[N_BUCKETS - 1, h]


def _bias_body(table_ref, bp_ref, bl_ref, bn_ref, *, tq, dt, past):
    h = pl.program_id(0)
    i = lax.broadcasted_iota(jnp.int32, (tq, tq), 0)
    j = lax.broadcasted_iota(jnp.int32, (tq, tq), 1)
    rel = i - j
    bp_ref[0] = jnp.where(rel >= 0, _t5_gather(table_ref, rel, h), NEG)
    bp_ref[1] = _t5_gather(table_ref, rel + tq, h)
    tt = lax.broadcasted_iota(jnp.int32, (8, LANES), 0)
    kk = lax.broadcasted_iota(jnp.int32, (8, LANES), 1)
    bl_ref[...] = _t5_gather(table_ref, LANES + tt - kk, h)
    reln = tt - kk
    bn_ref[...] = jnp.where((reln >= 0) & (kk < dt), _t5_gather(table_ref, reln, h), NEG)


def _bias_tiles(t5_table, tq, dt, past):
    return pl.pallas_call(
        functools.partial(_bias_body, tq=tq, dt=dt, past=past),
        grid=(NH,),
        in_specs=[pl.BlockSpec(memory_space=pltpu.SMEM)],
        out_specs=[
            pl.BlockSpec((None, 2, tq, tq), lambda h: (h, 0, 0, 0)),
            pl.BlockSpec((None, 8, LANES), lambda h: (h, 0, 0)),
            pl.BlockSpec((None, 8, LANES), lambda h: (h, 0, 0)),
        ],
        out_shape=[
            jax.ShapeDtypeStruct((NH, 2, tq, tq), F32),
            jax.ShapeDtypeStruct((NH, 8, LANES), F32),
            jax.ShapeDtypeStruct((NH, 8, LANES), F32),
        ],
        compiler_params=_cparams(("arbitrary",)),
        name="t5_bias",
    )(t5_table)


def _online_softmax(s, m_ref, l_ref, idx):
    m_prev = m_ref[idx]
    m_new = jnp.maximum(m_prev, jnp.max(s, axis=-1, keepdims=True))
    alpha = jnp.exp(m_prev - m_new)
    p = jnp.exp(s - m_new)
    l_ref[idx] = alpha * l_ref[idx] + jnp.sum(p, axis=-1, keepdims=True)
    m_ref[idx] = m_new
    return p.astype(BF16), alpha


def _lambda(lq_ref, lam_init):
    lq = lq_ref[...]
    s1 = jnp.sum(lq[0:1, :] * lq[1:2, :], axis=-1, keepdims=True)
    s2 = jnp.sum(lq[2:3, :] * lq[3:4, :], axis=-1, keepdims=True)
    return jnp.exp(s1) - jnp.exp(s2) + lam_init


def _head_rms(o, g, nheads):
    head = lax.broadcasted_iota(jnp.int32, (1, o.shape[1]), 1) // HEAD
    sq = o * o
    inv = jnp.zeros_like(o)
    for hh in range(nheads):
        ms = jnp.sum(jnp.where(head == hh, sq, 0.0), axis=-1, keepdims=True) * (1.0 / HEAD)
        inv = jnp.where(head == hh, lax.rsqrt(ms + EPS), inv)
    return o * inv * g


def _attn_p_body(q_ref, kt_ref, vt_ref, bias_ref, ck_ref, lq_ref, gd_ref, gf_ref, o_ref,
                 qm_ref, m_ref, l_ref, acc_ref, *, tq, lam_init):
    g = pl.program_id(1)
    qi = pl.program_id(2)
    lane = lax.broadcasted_iota(jnp.int32, (1, LANES), 1)
    row128 = lax.broadcasted_iota(jnp.int32, (LANES, 1), 0)

    m_ref[...] = jnp.full(m_ref.shape, NEG, F32)
    l_ref[...] = jnp.zeros_like(l_ref)
    acc_ref[...] = jnp.zeros_like(acc_ref)

    def masked_v(kj):
        vt = vt_ref[kj]
        return jnp.where(row128 < HEAD, vt, 0), jnp.where(row128 >= HEAD, vt, 0)

    @pl.when(g < 4)
    def _diff():
        q = q_ref[...]
        for hh in range(2):
            for c in range(2):
                lo = hh * HEAD + c * DQK
                qm_ref[2 * hh + c] = jnp.where((lane >= lo) & (lane < lo + DQK), q, 0)

        def step(kj, which):
            kt = kt_ref[kj]
            vt0, vt1 = masked_v(kj)
            for c in range(2):
                ps, alphas = [], []
                for hh in range(2):
                    s = _dot(qm_ref[2 * hh + c], kt)
                    if which is not None:
                        s = s + bias_ref[hh, which]
                    p, alpha = _online_softmax(s, m_ref, l_ref, 2 * hh + c)
                    ps.append(p)
                    alphas.append(alpha)
                pv = _dot_nt(ps[0], vt0) + _dot_nt(ps[1], vt1)
                acc_ref[c] = acc_ref[c] * jnp.where(lane < HEAD, alphas[0], alphas[1]) + pv

        def far(kj, carry):
            step(kj, None)
            return carry

        lax.fori_loop(0, qi - 1, far, 0)

        @pl.when(qi >= 1)
        def _():
            step(qi - 1, 1)

        step(qi, 0)

        lam = _lambda(lq_ref, lam_init)
        o0 = acc_ref[0] / jnp.where(lane < HEAD, l_ref[0], l_ref[2])
        o1 = acc_ref[1] / jnp.where(lane < HEAD, l_ref[1], l_ref[3])
        o = o0 - lam * o1
        o_ref[...] = _head_rms(o, gd_ref[:, :LANES], 2) * (1.0 - lam_init)

    @pl.when(g >= 4)
    def _fox():
        q = q_ref[...]
        for hh in range(2):
            qm_ref[hh] = jnp.where((lane >= hh * HEAD) & (lane < (hh + 1) * HEAD), q, 0)
        h0 = 2 * (g - 4)

        def step(kj, diag):
            kt = kt_ref[kj]
            vt0, vt1 = masked_v(kj)
            ps, alphas = [], []
            for hh in range(2):
                s = _dot(qm_ref[hh], kt) - ck_ref[kj, pl.ds(h0 + hh, 1), :]
                if diag:
                    r = lax.broadcasted_iota(jnp.int32, s.shape, 0)
                    cc = lax.broadcasted_iota(jnp.int32, s.shape, 1)
                    s = jnp.where(cc <= r, s, NEG)
                p, alpha = _online_softmax(s, m_ref, l_ref, hh)
                ps.append(p)
                alphas.append(alpha)
            pv = _dot_nt(ps[0], vt0) + _dot_nt(ps[1], vt1)
            acc_ref[0] = acc_ref[0] * jnp.where(lane < HEAD, alphas[0], alphas[1]) + pv

        def far(kj, carry):
            step(kj, False)
            return carry

        lax.fori_loop(0, qi, far, 0)
        step(qi, True)

        o = acc_ref[0] / jnp.where(lane < HEAD, l_ref[0], l_ref[1])
        o_ref[...] = _head_rms(o, gf_ref[:, :LANES], 2)


def _attn_p(q, kt, vt, bias_p, ck, lambda_qk, gd, gf, l, n_total, b, t, tq, lam_init):
    nq = t // tq
    ngrp = 2 * MIXW // LANES
    return pl.pallas_call(
        functools.partial(_attn_p_body, tq=tq, lam_init=lam_init),
        grid=(b, ngrp, nq),
        in_specs=[
            pl.BlockSpec((tq, LANES), lambda bi, g, qi: (bi * nq + qi, g)),
            pl.BlockSpec((None, nq, LANES, tq), lambda bi, g, qi: (bi, 0, g, 0)),
            pl.BlockSpec((None, nq, LANES, tq), lambda bi, g, qi: (bi, 0, g, 0)),
            pl.BlockSpec((2, 2, tq, tq), lambda bi, g, qi: (jnp.minimum(g, 3), 0, 0, 0)),
            pl.BlockSpec((None, nq, NH, tq), lambda bi, g, qi: (bi, 0, 0, 0)),
            pl.BlockSpec((None, 4, DQK), lambda bi, g, qi: (l, 0, 0)),
            pl.BlockSpec((None, 1, MIXW), lambda bi, g, qi: (l, 0, 0)),
            pl.BlockSpec((None, 1, MIXW), lambda bi, g, qi: (l, 0, 0)),
        ],
        out_specs=pl.BlockSpec((tq, LANES), lambda bi, g, qi: (bi * nq + qi, g)),
        out_shape=jax.ShapeDtypeStruct((n_total, 2 * MIXW), F32),
        scratch_shapes=[
            pltpu.VMEM((4, tq, LANES), BF16),
            pltpu.VMEM((4, tq, 1), F32),
            pltpu.VMEM((4, tq, 1), F32),
            pltpu.VMEM((2, tq, LANES), F32),
        ],
        compiler_params=_cparams(("parallel", "parallel", "arbitrary")),
        name="attn_prompt",
    )(q, kt, vt, bias_p, ck, lambda_qk, gd, gf)


def _attn_s_body(*refs, npg, nsteps, dt, lam_init):
    pt_ref = refs[0]
    (q_ref, ndk_ref, ndv_ref, nfk_ref, nfv_ref, lfn_ref, bl_ref, bn_ref, lq_ref, gd_ref, gf_ref) = refs[1:12]
    page_refs = refs[12:12 + 5 * npg]
    _mix_in = refs[12 + 5 * npg]
    o_ref = refs[13 + 5 * npg]
    (qd_ref, qf_ref, md_ref, ld_ref, accd_ref, mf_ref, lf_ref, accf_ref, carry_ref) = refs[14 + 5 * npg:]
    j = pl.program_id(1)
    nd = 2 * NH * dt
    nf = NH * dt

    @pl.when(j == 0)
    def _init():
        q = q_ref[...]
        lane = lax.broadcasted_iota(jnp.int32, (1, MIXW), 1)
        rd = lax.broadcasted_iota(jnp.int32, (nd, 1), 0)
        qd = jnp.concatenate([q[:, :MIXW]] * (2 * NH), axis=0)
        keep = (lane // HEAD == (rd // dt) % NH) & ((lane // DQK) % 2 == rd // (NH * dt))
        qd_ref[...] = jnp.where(keep, qd, 0.0).astype(BF16)
        rf = lax.broadcasted_iota(jnp.int32, (nf, 1), 0)
        qf = jnp.concatenate([q[:, MIXW:]] * NH, axis=0)
        qf_ref[...] = jnp.where(lane // HEAD == rf // dt, qf, 0.0).astype(BF16)
        md_ref[...] = jnp.full(md_ref.shape, NEG, F32)
        ld_ref[...] = jnp.zeros_like(ld_ref)
        accd_ref[...] = jnp.zeros_like(accd_ref)
        mf_ref[...] = jnp.full(mf_ref.shape, NEG, F32)
        lf_ref[...] = jnp.zeros_like(lf_ref)
        accf_ref[...] = jnp.zeros_like(accf_ref)
        carry_ref[...] = jnp.zeros_like(carry_ref)

    def update(s, m_ref, l_ref, acc_ref, v_op, nt):
        m_prev = m_ref[...]
        m_new = jnp.maximum(m_prev, jnp.max(s, axis=-1, keepdims=True))
        alpha = jnp.exp(m_prev - m_new)
        p = jnp.exp(s - m_new)
        l_ref[...] = alpha * l_ref[...] + jnp.sum(p, axis=-1, keepdims=True)
        m_ref[...] = m_new
        pb = p.astype(BF16)
        pv = _dot_nt(pb, v_op) if nt else _dot(pb, v_op)
        acc_ref[...] = acc_ref[...] * alpha + pv

    tr = lax.broadcasted_iota(jnp.int32, (LANES, LANES), 0)
    tc = lax.broadcasted_iota(jnp.int32, (LANES, LANES), 1)
    tri = jnp.where(tr <= tc, 1.0, 0.0).astype(BF16)

    def prefix_rows(lf):
        rows = jnp.concatenate(
            [jnp.broadcast_to(lf[hh:hh + 1, :], (dt, LANES)) for hh in range(NH)], axis=0)
        hi, mid, lo = _split3(rows)
        return _dot(hi, tri) + _dot(mid, tri) + _dot(lo, tri)

    bl = jnp.concatenate([bl_ref[...]] * 2, axis=0)
    for p in range(npg):
        cdk, cdv, cfk, cfv, clf = page_refs[5 * p:5 * p + 5]
        s = _dot(qd_ref[...], cdk[...].astype(BF16))
        if p == npg - 1:
            s = s + jnp.where(j == nsteps - 1, bl, 0.0)
        update(s, md_ref, ld_ref, accd_ref, cdv[...].astype(BF16), True)

        ck = prefix_rows(clf[...]) + carry_ref[...]
        carry_ref[...] = ck[:, LANES - 1:LANES]
        s = _dot(qf_ref[...], cfk[...].astype(BF16)) - ck
        update(s, mf_ref, lf_ref, accf_ref, cfv[...].astype(BF16), True)

    @pl.when(j == nsteps - 1)
    def _final():
        def pad_rows(x):
            return jnp.concatenate([x, jnp.zeros((LANES - dt, x.shape[1]), F32)], axis=0).astype(BF16)

        s = _dot_nt(qd_ref[...], pad_rows(ndk_ref[...]))
        s = s + jnp.concatenate([bn_ref[...]] * 2, axis=0)
        update(s, md_ref, ld_ref, accd_ref, pad_rows(ndv_ref[...]), False)

        ckn = prefix_rows(lfn_ref[...]) + carry_ref[...]
        rt = lax.broadcasted_iota(jnp.int32, (nf, LANES), 0) % dt
        kk = lax.broadcasted_iota(jnp.int32, (nf, LANES), 1)
        s = _dot_nt(qf_ref[...], pad_rows(nfk_ref[...])) - ckn
        s = jnp.where((kk <= rt) & (kk < dt), s, NEG)
        update(s, mf_ref, lf_ref, accf_ref, pad_rows(nfv_ref[...]), False)

        head = lax.broadcasted_iota(jnp.int32, (1, MIXW), 1) // HEAD

        def block_diag(acc, l, base):
            out = jnp.zeros((dt, MIXW), F32)
            for hh in range(NH):
                r0 = base + hh * dt
                out = out + jnp.where(head == hh, acc[r0:r0 + dt, :] / l[r0:r0 + dt, :], 0.0)
            return out

        accd, ld = accd_ref[...], ld_ref[...]
        lam = _lambda(lq_ref, lam_init)
        od = block_diag(accd, ld, 0) - lam * block_diag(accd, ld, NH * dt)
        od = _head_rms(od, gd_ref[...], NH) * (1.0 - lam_init)
        of = _head_rms(block_diag(accf_ref[...], lf_ref[...], 0), gf_ref[...], NH)
        o_ref[...] = jnp.concatenate([od, of], axis=1)


def _attn_s(pt_flat, q_s, rows_s, lfn, bias_last, bias_new, lambda_qk, gd, gf, caches, mix, l,
            n_prompt, db, dt, npages, npg, lam_init):
    nsteps = npages // npg
    cdk, cdv, cfk, cfv, clf = caches
    feat, ps = cdk.shape[2], cdk.shape[3]
    in_specs = [
        pl.BlockSpec((dt, 2 * MIXW), lambda b, j, pt: (b, 0)),
    ] + [pl.BlockSpec((None, dt, MIXW), lambda b, j, pt: (l, b, 0))] * 4 + [
        pl.BlockSpec((None, NH, LANES), lambda b, j, pt: (b, 0, 0)),
        pl.BlockSpec((NH * dt, LANES), lambda b, j, pt: (0, 0)),
        pl.BlockSpec((NH * dt, LANES), lambda b, j, pt: (0, 0)),
        pl.BlockSpec((None, 4, DQK), lambda b, j, pt: (l, 0, 0)),
        pl.BlockSpec((None, 1, MIXW), lambda b, j, pt: (l, 0, 0)),
        pl.BlockSpec((None, 1, MIXW), lambda b, j, pt: (l, 0, 0)),
    ]
    args = [q_s] + list(rows_s[:4]) + [lfn, bias_last, bias_new, lambda_qk, gd, gf]
    for p in range(npg):
        def page_idx(b, j, pt, p=p):
            return (l, pt[b * npages + j * npg + p], 0, 0)
        in_specs += [pl.BlockSpec((None, None, feat, ps), page_idx)] * 4
        in_specs += [pl.BlockSpec((None, None, NH, ps), page_idx)]
        args += [cdk, cdv, cfk, cfv, clf]
    in_specs.append(pl.BlockSpec(memory_space=pl.ANY))
    args.append(mix)
    nd, nf = 2 * NH * dt, NH * dt
    return pl.pallas_call(
        functools.partial(_attn_s_body, npg=npg, nsteps=nsteps, dt=dt, lam_init=lam_init),
        grid_spec=pltpu.PrefetchScalarGridSpec(
            num_scalar_prefetch=1,
            grid=(db, nsteps),
            in_specs=in_specs,
            out_specs=pl.BlockSpec((dt, 2 * MIXW), lambda b, j, pt: (n_prompt // dt + b, 0)),
            scratch_shapes=[
                pltpu.VMEM((nd, MIXW), BF16), pltpu.VMEM((nf, MIXW), BF16),
                pltpu.VMEM((nd, 1), F32), pltpu.VMEM((nd, 1), F32), pltpu.VMEM((nd, MIXW), F32),
                pltpu.VMEM((nf, 1), F32), pltpu.VMEM((nf, 1), F32), pltpu.VMEM((nf, MIXW), F32),
                pltpu.VMEM((nf, 1), F32),
            ],
        ),
        out_shape=jax.ShapeDtypeStruct(mix.shape, mix.dtype),
        input_output_aliases={len(args): 0},
        compiler_params=_cparams(("parallel", "arbitrary")),
        name="attn_sample",
    )(pt_flat, *args)


def _largest_tile(n, candidates):
    for c in candidates:
        if all(x % c == 0 for x in n):
            return c
    raise ValueError(f"no tile in {candidates} divides {n}")


def kernel(x_prompt, x_sample, cache_diff_k, cache_diff_v, cache_fox_k, cache_fox_v, cache_fox_logf,
           page_table, norm_g, w_ffn_in, w_ffn_out, w_in, b_forget, lambda_qk, diff_norm_g, fox_norm_g,
           w_out, t5_table):
    b, t, d = x_prompt.shape
    db, dt, _ = x_sample.shape
    depth, npool, ps = cache_diff_k.shape[:3]
    npages = page_table.shape[1]
    past = npages * ps
    n_p, n_s = b * t, db * dt
    n = n_p + n_s
    assert ps == LANES and dt == 8 and cache_diff_k.shape[3:] == (NH, HEAD)

    tm = _largest_tile((n_p, n_s), (512, 256, 128))
    tq = _largest_tile((t,), (256, 128))
    tm_p = _largest_tile((t,), (512, 256, 128))
    tm_s = min(tm, n_s)
    npg = _largest_tile((npages,), (4, 2, 1))

    wi = w_ffn_in.astype(BF16)
    wo = w_ffn_out.astype(BF16)
    wout = w_out.astype(BF16)
    o_dk, o_dv, o_fq, o_fk, o_fv, o_ff = (MIXW * k for k in range(1, 7))
    wq = jnp.concatenate([w_in[:, :, :o_dk], w_in[:, :, o_fq:o_fk]], axis=2).astype(BF16)
    wkv = jnp.concatenate([w_in[:, :, o_dk:o_fq], w_in[:, :, o_fk:o_ff]], axis=2).astype(BF16)
    wkvt = jnp.swapaxes(wkv, 1, 2)
    wf = jnp.pad(w_in[:, :, o_ff:], ((0, 0), (0, 0), (0, LANES - NH))).astype(BF16)
    wft = jnp.swapaxes(wf[:, :, :16], 1, 2)
    bfrow = jnp.pad(b_forget, ((0, 0), (0, LANES - NH)))[:, None, :]
    bfcol = jnp.pad(b_forget, ((0, 0), (0, 16 - NH)))[:, :, None]
    gd = jnp.tile(diff_norm_g, (1, NH))[:, None, :]
    gf = jnp.tile(fox_norm_g, (1, NH))[:, None, :]
    pt_flat = page_table.reshape(-1)

    def page_major(c):
        return jnp.transpose(c, (0, 1, 3, 4, 2)).reshape(depth, npool, NH * HEAD, ps)
    caches = (page_major(cache_diff_k), page_major(cache_diff_v), page_major(cache_fox_k),
              page_major(cache_fox_v), jnp.transpose(cache_fox_logf, (0, 1, 3, 2)))

    bias_p, bias_last, bias_new = _bias_tiles(t5_table, tq, dt, past)
    bias_last = bias_last.reshape(NH * dt, LANES)
    bias_new = bias_new.reshape(NH * dt, LANES)

    h = jnp.concatenate([x_prompt.reshape(n_p, d), x_sample.reshape(n_s, d)], axis=0)
    rows_p = rows_s = None
    for l in range(depth):
        lam_init = 0.8 - 0.6 * math.exp(-0.3 * l)
        h = _ffn_a(h, norm_g, wi, wo, l, tm)
        outs = _proj_p(h, 0, rows_p, norm_g, wq, wkvt, wft, bfcol, l, depth, b, t, tm_p, tq)
        rows_p, (q_p, kt, vt, ck) = outs[:5], outs[5:]
        outs = _proj_s(h, n_p // tm_s, rows_s, norm_g, wq, wkv, wf, bfrow, l, depth, n_s, tm_s)
        rows_s, q_s = outs[:5], outs[5]
        lfn = jnp.transpose(rows_s[4][l].reshape(db, dt, NH), (0, 2, 1))
        lfn = jnp.pad(lfn, ((0, 0), (0, 0), (0, LANES - dt)))
        mix = _attn_p(q_p, kt, vt, bias_p, ck, lambda_qk, gd, gf, l, n, b, t, tq, lam_init)
        mix = _attn_s(pt_flat, q_s, rows_s, lfn, bias_last, bias_new, lambda_qk, gd, gf, caches, mix, l,
                      n_p, db, dt, npages, npg, lam_init)
        h = _ffn_b(h, mix, norm_g, wout, wi, wo, l, tm)

    def rows_out_p(r):
        return jnp.transpose(r.reshape(depth, b, NH, HEAD, t), (0, 1, 4, 2, 3))

    def rows_out_s(r):
        return r.reshape(depth, db, dt, NH, HEAD)

    return (h[:n_p].reshape(b, t, d), h[n_p:].reshape(db, dt, d),
            rows_out_p(rows_p[0]), rows_out_p(rows_p[1]), rows_out_p(rows_p[2]), rows_out_p(rows_p[3]),
            jnp.transpose(rows_p[4], (0, 1, 3, 2)),
            rows_out_s(rows_s[0]), rows_out_s(rows_s[1]), rows_out_s(rows_s[2]), rows_out_s(rows_s[3]),
            rows_s[4].reshape(depth, db, dt, NH))
```

```python
import functools
import math

import jax
import jax.numpy as jnp
from jax import lax
from jax.experimental import pallas as pl
from jax.experimental.pallas import tpu as pltpu

F32 = jnp.float32
BF16 = jnp.bfloat16
EPS = 1e-6
NEG = -1e30
HEAD = 64
DQK = 32
NH = 8
MIXW = NH * HEAD
N_BUCKETS = 32
MAX_DISTANCE = 128
LANES = 128
LOG2E = math.log2(math.e)
VMEM_LIMIT = 56 * 1024 * 1024
FAR_CHUNK = 2


def _cparams(sem):
    return pltpu.CompilerParams(dimension_semantics=sem, vmem_limit_bytes=VMEM_LIMIT)


def _rms(x, g):
    ms = jnp.mean(x * x, axis=-1, keepdims=True)
    return x * lax.rsqrt(ms + EPS) * g


def _dot(a, b):
    return jnp.dot(a, b, preferred_element_type=F32)


def _dot_nt(a, b):
    return lax.dot_general(a, b, (((1,), (1,)), ((), ())), preferred_element_type=F32)


def _split3(x):
    hi = x.astype(BF16)
    r1 = x - hi.astype(F32)
    mid = r1.astype(BF16)
    lo = (r1 - mid.astype(F32)).astype(BF16)
    return hi, mid, lo


def _log_sigmoid(x):
    return jnp.minimum(x, 0.0) - jnp.log1p(jnp.exp(-jnp.abs(x)))


def _const_spec(block, index):
    return pl.BlockSpec(block, index, pipeline_mode=pl.Buffered(1))


def _swiglu_tile(a_ref, h_ref, wi_ref, wo_ref, d_ff, chunk):
    for c0 in range(0, d_ff, chunk):
        a = a_ref[...]
        gate = _dot(a, wi_ref[:, c0:c0 + chunk])
        up = _dot(a, wi_ref[:, d_ff + c0:d_ff + c0 + chunk])
        h_ref[:, c0:c0 + chunk] = (gate * jax.nn.sigmoid(gate) * up).astype(BF16)
    return _dot(h_ref[...], wo_ref[...])


def _ffn_a_body(x_ref, g_ref, wi_ref, wo_ref, o_ref, a_ref, h_ref, *, d_ff, chunk):
    x = x_ref[...]
    a_ref[...] = _rms(x, g_ref[0:1, :]).astype(BF16)
    y = _swiglu_tile(a_ref, h_ref, wi_ref, wo_ref, d_ff, chunk)
    o_ref[...] = x + 0.5 * _rms(y, g_ref[1:2, :])


def _ffn_a(x, norm_g, wi, wo, l, tm):
    n, d = x.shape
    d_ff = wo.shape[2]
    chunk = 256 if d_ff % 256 == 0 else LANES
    return pl.pallas_call(
        functools.partial(_ffn_a_body, d_ff=d_ff, chunk=chunk),
        grid=(n // tm,),
        in_specs=[
            pl.BlockSpec((tm, d), lambda i: (i, 0)),
            _const_spec((None, 6, d), lambda i: (l, 0, 0)),
            _const_spec((None, None, d, 2 * d_ff), lambda i: (l, 0, 0, 0)),
            _const_spec((None, None, d_ff, d), lambda i: (l, 0, 0, 0)),
        ],
        out_specs=pl.BlockSpec((tm, d), lambda i: (i, 0)),
        out_shape=jax.ShapeDtypeStruct((n, d), F32),
        scratch_shapes=[pltpu.VMEM((tm, d), BF16), pltpu.VMEM((tm, d_ff), BF16)],
        compiler_params=_cparams(("parallel",)),
        name="ffn_a",
    )(x, norm_g, wi, wo)


def _ffn_b_body(x_ref, mix_ref, g_ref, wout_ref, wi_ref, wo_ref, o_ref, a_ref, h_ref, *, d_ff, chunk):
    m = _dot(mix_ref[...].astype(BF16), wout_ref[...])
    h2 = x_ref[...] + _rms(m, g_ref[3:4, :])
    a_ref[...] = _rms(h2, g_ref[4:5, :]).astype(BF16)
    y = _swiglu_tile(a_ref, h_ref, wi_ref, wo_ref, d_ff, chunk)
    o_ref[...] = h2 + 0.5 * _rms(y, g_ref[5:6, :])


def _ffn_b(x, mix, norm_g, w_out, wi, wo, l, tm):
    n, d = x.shape
    d_ff = wo.shape[2]
    chunk = 256 if d_ff % 256 == 0 else LANES
    return pl.pallas_call(
        functools.partial(_ffn_b_body, d_ff=d_ff, chunk=chunk),
        grid=(n // tm,),
        in_specs=[
            pl.BlockSpec((tm, d), lambda i: (i, 0)),
            pl.BlockSpec((tm, mix.shape[1]), lambda i: (i, 0)),
            _const_spec((None, 6, d), lambda i: (l, 0, 0)),
            _const_spec((None, mix.shape[1], d), lambda i: (l, 0, 0)),
            _const_spec((None, None, d, 2 * d_ff), lambda i: (l, 1, 0, 0)),
            _const_spec((None, None, d_ff, d), lambda i: (l, 1, 0, 0)),
        ],
        out_specs=pl.BlockSpec((tm, d), lambda i: (i, 0)),
        out_shape=jax.ShapeDtypeStruct((n, d), F32),
        scratch_shapes=[pltpu.VMEM((tm, d), BF16), pltpu.VMEM((tm, d_ff), BF16)],
        compiler_params=_cparams(("parallel",)),
        name="ffn_b",
    )(x, mix, norm_g, w_out, wi, wo)


def _proj_p_body(*refs, tm, tk, aliased):
    if aliased:
        refs = refs[5:]
    (x_ref, g_ref, wqt_ref, wkvt_ref, wk_ref, wft_ref, bfc_ref, wf_ref, bfr_ref,
     dk_ref, dv_ref, fk_ref, fv_ref, lf_ref, qt_ref, kd_ref, kf_ref, vt_ref, carry_ref) = refs
    ti = pl.program_id(1)
    nsub = tm // tk
    a = _rms(x_ref[...], g_ref[2:3, :]).astype(BF16)

    qt = _dot_nt(wqt_ref[...], a)
    frow = lax.broadcasted_iota(jnp.int32, (qt.shape[0], 1), 0)
    qt = (qt * jnp.where(frow < MIXW, DQK ** -0.5 * LOG2E, HEAD ** -0.5 * LOG2E)).astype(BF16)
    for j in range(nsub):
        qt_ref[j] = qt[:, j * tk:(j + 1) * tk]

    for s, (row_ref, off) in enumerate(((dk_ref, None), (dv_ref, 0), (fk_ref, None), (fv_ref, MIXW))):
        sec = _dot_nt(wkvt_ref[s * MIXW:(s + 1) * MIXW, :], a)
        row_ref[...] = sec
        if off is not None:
            secb = sec.astype(BF16)
            for j in range(nsub):
                vt_ref[j, off:off + MIXW, :] = secb[:, j * tk:(j + 1) * tk]

    kd_ref[...] = _dot(a, wk_ref[:, :MIXW]).astype(BF16)
    fk = _dot(a, wk_ref[:, MIXW:]).astype(BF16)

    lf_ref[...] = _log_sigmoid(_dot_nt(wft_ref[...], a) + bfc_ref[...])[:NH, :]

    lane = lax.broadcasted_iota(jnp.int32, (1, LANES), 1)
    lfn = jnp.where(lane < NH, _log_sigmoid(_dot(a, wf_ref[...]) + bfr_ref[...]), 0.0)

    @pl.when(ti == 0)
    def _():
        carry_ref[...] = jnp.zeros_like(carry_ref)

    r = lax.broadcasted_iota(jnp.int32, (tm, tm), 0)
    c = lax.broadcasted_iota(jnp.int32, (tm, tm), 1)
    tril = jnp.where(r >= c, 1.0, 0.0).astype(BF16)
    hi, mid, lo = _split3(lfn)
    cs = _dot(tril, hi) + _dot(tril, mid) + _dot(tril, lo) + carry_ref[...]
    carry_ref[...] = cs[tm - 1:tm, :]

    pr = lax.broadcasted_iota(jnp.int32, (LANES, MIXW), 0)
    pc = lax.broadcasted_iota(jnp.int32, (LANES, MIXW), 1)
    base = LANES * (pr // 2) + 3 * (pr % 2)
    aug = jnp.zeros((tm, MIXW), F32)
    for k, part in enumerate(_split3(cs * LOG2E)):
        place = jnp.where((pr < NH) & (pc == base + k), 1.0, 0.0).astype(BF16)
        aug = aug + _dot(part, place)
    aug = aug.astype(BF16)
    pieces = []
    for grp in range(MIXW // LANES):
        pieces += [fk[:, grp * LANES:(grp + 1) * LANES], aug[:, grp * LANES:(grp + 1) * LANES]]
    kf_ref[...] = jnp.concatenate(pieces, axis=1)


def _proj_p(h, prev_rows, norm_g, wqt, wkvt, wk, wft, bfcol, wf, bfrow, l, depth, b, t, tm, tk):
    d = h.shape[1]
    nt = t // tm
    nsub = tm // tk
    nk = t // tk
    aliased = prev_rows is not None
    row_shape = jax.ShapeDtypeStruct((depth, b, MIXW, t), F32)
    out_shape = [row_shape] * 4 + [
        jax.ShapeDtypeStruct((depth, b, NH, t), F32),
        jax.ShapeDtypeStruct((b, nk, 2 * MIXW, tk), BF16),
        jax.ShapeDtypeStruct((b * t, MIXW), BF16),
        jax.ShapeDtypeStruct((b * t, 2 * MIXW), BF16),
        jax.ShapeDtypeStruct((b, nk, 2 * MIXW, tk), BF16),
    ]
    row_spec = pl.BlockSpec((None, None, MIXW, tm), lambda bi, ti: (l, bi, 0, ti))
    out_specs = [row_spec] * 4 + [
        pl.BlockSpec((None, None, NH, tm), lambda bi, ti: (l, bi, 0, ti)),
        pl.BlockSpec((None, nsub, 2 * MIXW, tk), lambda bi, ti: (bi, ti, 0, 0)),
        pl.BlockSpec((tm, MIXW), lambda bi, ti: (bi * nt + ti, 0)),
        pl.BlockSpec((tm, 2 * MIXW), lambda bi, ti: (bi * nt + ti, 0)),
        pl.BlockSpec((None, nsub, 2 * MIXW, tk), lambda bi, ti: (bi, ti, 0, 0)),
    ]
    in_specs = [
        pl.BlockSpec((tm, d), lambda bi, ti: (bi * nt + ti, 0)),
        _const_spec((None, 6, d), lambda bi, ti: (l, 0, 0)),
        _const_spec((None, 2 * MIXW, d), lambda bi, ti: (l, 0, 0)),
        _const_spec((None, 4 * MIXW, d), lambda bi, ti: (l, 0, 0)),
        _const_spec((None, d, 2 * MIXW), lambda bi, ti: (l, 0, 0)),
        _const_spec((None, 16, d), lambda bi, ti: (l, 0, 0)),
        _const_spec((None, 16, 1), lambda bi, ti: (l, 0, 0)),
        _const_spec((None, d, LANES), lambda bi, ti: (l, 0, 0)),
        _const_spec((None, 1, LANES), lambda bi, ti: (l, 0, 0)),
    ]
    args = [h, norm_g, wqt, wkvt, wk, wft, bfcol, wf, bfrow]
    aliases = {}
    if aliased:
        in_specs = [pl.BlockSpec(memory_space=pl.ANY)] * 5 + in_specs
        args = list(prev_rows) + args
        aliases = {i: i for i in range(5)}
    return pl.pallas_call(
        functools.partial(_proj_p_body, tm=tm, tk=tk, aliased=aliased),
        grid=(b, nt),
        in_specs=in_specs,
        out_specs=out_specs,
        out_shape=out_shape,
        scratch_shapes=[pltpu.VMEM((1, LANES), F32)],
        input_output_aliases=aliases,
        compiler_params=_cparams(("parallel", "arbitrary")),
        name="proj_prompt",
    )(*args)


def _proj_s_body(*refs, aliased):
    if aliased:
        refs = refs[5:]
    (x_ref, g_ref, wq_ref, wkv_ref, wf_ref, bf_ref,
     dk_ref, dv_ref, fk_ref, fv_ref, lf_ref, q_ref) = refs
    a = _rms(x_ref[...], g_ref[2:3, :]).astype(BF16)
    q = _dot(a, wq_ref[...])
    lane = lax.broadcasted_iota(jnp.int32, (1, q.shape[1]), 1)
    q_ref[...] = q * jnp.where(lane < MIXW, DQK ** -0.5 * LOG2E, HEAD ** -0.5 * LOG2E)
    for s, row_ref in enumerate((dk_ref, dv_ref, fk_ref, fv_ref)):
        row_ref[...] = _dot(a, wkv_ref[:, s * MIXW:(s + 1) * MIXW])
    pf = _dot(a, wf_ref[...]) + bf_ref[...]
    lf_ref[...] = _log_sigmoid(pf)[:, :NH]


def _proj_s(h, row_off_blocks, prev_rows, norm_g, wq, wkv, wf, bfrow, l, depth, ns, tm):
    d = h.shape[1]
    aliased = prev_rows is not None
    row_shape = jax.ShapeDtypeStruct((depth, ns, MIXW), F32)
    out_shape = [row_shape] * 4 + [
        jax.ShapeDtypeStruct((depth, ns, NH), F32),
        jax.ShapeDtypeStruct((ns, 2 * MIXW), F32),
    ]
    row_spec = pl.BlockSpec((None, tm, MIXW), lambda i: (l, i, 0))
    out_specs = [row_spec] * 4 + [
        pl.BlockSpec((None, tm, NH), lambda i: (l, i, 0)),
        pl.BlockSpec((tm, 2 * MIXW), lambda i: (i, 0)),
    ]
    in_specs = [
        pl.BlockSpec((tm, d), lambda i: (row_off_blocks + i, 0)),
        _const_spec((None, 6, d), lambda i: (l, 0, 0)),
        _const_spec((None, d, 2 * MIXW), lambda i: (l, 0, 0)),
        _const_spec((None, d, 4 * MIXW), lambda i: (l, 0, 0)),
        _const_spec((None, d, LANES), lambda i: (l, 0, 0)),
        _const_spec((None, 1, LANES), lambda i: (l, 0, 0)),
    ]
    args = [h, norm_g, wq, wkv, wf, bfrow]
    aliases = {}
    if aliased:
        in_specs = [pl.BlockSpec(memory_space=pl.ANY)] * 5 + in_specs
        args = list(prev_rows) + args
        aliases = {i: i for i in range(5)}
    return pl.pallas_call(
        functools.partial(_proj_s_body, aliased=aliased),
        grid=(ns // tm,),
        in_specs=in_specs,
        out_specs=out_specs,
        out_shape=out_shape,
        input_output_aliases=aliases,
        compiler_params=_cparams(("parallel",)),
        name="proj_sample",
    )(*args)


def _t5_bucket(rel):
    n = jnp.maximum(rel, 0)
    max_exact = N_BUCKETS // 2
    nf = jnp.maximum(n, 1).astype(F32)
    large = max_exact + (jnp.log(nf / max_exact) / math.log(MAX_DISTANCE / max_exact)
                         * (N_BUCKETS - max_exact)).astype(jnp.int32)
    large = jnp.minimum(large, N_BUCKETS - 1)
    return jnp.where(n < max_exact, n, large)


def _t5_gather(table_ref, rel, h):
    bucket = _t5_bucket(rel)
    acc = jnp.zeros(rel.shape, F32)
    for bkt in range(N_BUCKETS):
        acc = jnp.where(bucket == bkt, table_ref[bkt, h], acc)
    return (acc - table_ref[N_BUCKETS - 1, h]) * LOG2E


def _bias_body(table_ref, bp_ref, bl_ref, bn_ref, *, tq, dt):
    h = pl.program_id(0)
    k = lax.broadcasted_iota(jnp.int32, (tq, tq), 0)
    q = lax.broadcasted_iota(jnp.int32, (tq, tq), 1)
    rel = q - k
    bp_ref[:tq, :] = _t5_gather(table_ref, rel + tq, h)
    bp_ref[tq:, :] = jnp.where(rel >= 0, _t5_gather(table_ref, rel, h), NEG)
    tt = lax.broadcasted_iota(jnp.int32, (8, LANES), 0)
    kk = lax.broadcasted_iota(jnp.int32, (8, LANES), 1)
    bl_ref[...] = _t5_gather(table_ref, LANES + tt - kk, h)
    reln = tt - kk
    bn_ref[...] = jnp.where((reln >= 0) & (kk < dt), _t5_gather(table_ref, reln, h), NEG)


def _bias_tiles(t5_table, tq, dt):
    return pl.pallas_call(
        functools.partial(_bias_body, tq=tq, dt=dt),
        grid=(NH,),
        in_specs=[pl.BlockSpec(memory_space=pltpu.SMEM)],
        out_specs=[
            pl.BlockSpec((None, 2 * tq, tq), lambda h: (h, 0, 0)),
            pl.BlockSpec((None, 8, LANES), lambda h: (h, 0, 0)),
            pl.BlockSpec((None, 8, LANES), lambda h: (h, 0, 0)),
        ],
        out_shape=[
            jax.ShapeDtypeStruct((NH, 2 * tq, tq), F32),
            jax.ShapeDtypeStruct((NH, 8, LANES), F32),
            jax.ShapeDtypeStruct((NH, 8, LANES), F32),
        ],
        compiler_params=_cparams(("arbitrary",)),
        name="t5_bias",
    )(t5_table)


def _lambda(lq_ref, lam_init):
    lq = lq_ref[...]
    s1 = jnp.sum(lq[0:1, :] * lq[1:2, :], axis=-1, keepdims=True)
    s2 = jnp.sum(lq[2:3, :] * lq[3:4, :], axis=-1, keepdims=True)
    return jnp.exp(s1) - jnp.exp(s2) + lam_init


def _head_rms(o, g, nheads):
    head = lax.broadcasted_iota(jnp.int32, (1, o.shape[1]), 1) // HEAD
    sq = o * o
    inv = jnp.zeros_like(o)
    for hh in range(nheads):
        ms = jnp.sum(jnp.where(head == hh, sq, 0.0), axis=-1, keepdims=True) * (1.0 / HEAD)
        inv = jnp.where(head == hh, lax.rsqrt(ms + EPS), inv)
    return o * inv * g


def _attn_p_body(qt_ref, kd_ref, kf_ref, vt_ref, bias_ref, lq_ref, gd_ref, gf_ref, o_ref,
                 qmd_ref, qmf_ref, m_ref, l_ref, acc_ref, *, tq, lam_init):
    g = pl.program_id(1)
    qi = pl.program_id(2)
    frow = lax.broadcasted_iota(jnp.int32, (LANES, 1), 0)

    m_ref[...] = jnp.full(m_ref.shape, NEG, F32)
    l_ref[...] = jnp.zeros_like(l_ref)
    acc_ref[...] = jnp.zeros_like(acc_ref)

    def update(sts, v_hs):
        n = len(sts)
        m_prev = [m_ref[i] for i in range(n)]
        l_prev = [l_ref[i] for i in range(n)]
        m_new = [jnp.maximum(m_prev[i], jnp.max(sts[i], axis=0, keepdims=True)) for i in range(n)]
        alpha = [jnp.exp2(m_prev[i] - m_new[i]) for i in range(n)]
        ps = [jnp.exp2(sts[i] - m_new[i]) for i in range(n)]
        pvs = [_dot(v_hs[i], ps[i].astype(BF16)) for i in range(n)]
        l_new = [alpha[i] * l_prev[i] + jnp.sum(ps[i], axis=0, keepdims=True) for i in range(n)]
        acc_new = [acc_ref[i] * alpha[i] + pvs[i] for i in range(n)]
        for i in range(n):
            m_ref[i] = m_new[i]
            l_ref[i] = l_new[i]
            acc_ref[i] = acc_new[i]

    def k_rows(ref, kj, ntiles):
        return ref[pl.ds(pl.multiple_of(kj * tq, tq), ntiles * tq), :]

    def v_cols(kj, ntiles):
        return jnp.concatenate([vt_ref[kj + i] for i in range(ntiles)], axis=1)

    def far_tiles(step, nfar):
        def body(c, carry):
            step(c * FAR_CHUNK, FAR_CHUNK)
            return carry
        lax.fori_loop(0, nfar // FAR_CHUNK, body, 0)
        for rem in range(1, FAR_CHUNK):
            @pl.when(nfar % FAR_CHUNK == rem)
            def _(rem=rem):
                step(nfar - rem, rem)

    @pl.when(g < 4)
    def _diff():
        qt = qt_ref[...]
        for hh in range(2):
            for c in range(2):
                lo = hh * HEAD + c * DQK
                qmd_ref[2 * hh + c] = jnp.where((frow >= lo) & (frow < lo + DQK), qt, 0)

        def step(kj, ntiles, biased=False):
            kt = k_rows(kd_ref, kj, ntiles)
            vt = v_cols(kj, ntiles)
            sts = [_dot(kt, qmd_ref[i]) for i in range(4)]
            if biased:
                sts = [sts[i] + bias_ref[i // 2, (2 - ntiles) * tq:, :] for i in range(4)]
            update(sts, [vt[(i // 2) * HEAD:(i // 2 + 1) * HEAD, :] for i in range(4)])

        far_tiles(step, jnp.maximum(qi - 1, 0))

        @pl.when(qi >= 1)
        def _():
            step(qi - 1, 2, True)

        @pl.when(qi == 0)
        def _():
            step(qi, 1, True)

        lam = _lambda(lq_ref, lam_init)
        outs = []
        for hh in range(2):
            o = acc_ref[2 * hh] / l_ref[2 * hh] - lam * (acc_ref[2 * hh + 1] / l_ref[2 * hh + 1])
            ms = jnp.mean(o * o, axis=0, keepdims=True)
            outs.append(o * lax.rsqrt(ms + EPS) * gd_ref[...] * (1.0 - lam_init))
        o_ref[...] = jnp.concatenate(outs, axis=0).T

    @pl.when(g >= 4)
    def _fox():
        qt = qt_ref[...]
        for hh in range(2):
            own = jnp.where((frow >= hh * HEAD) & (frow < (hh + 1) * HEAD), qt, 0)
            dec = jnp.where((frow >= 3 * hh) & (frow < 3 * hh + 3), -1.0, 0.0)
            qmf_ref[hh] = jnp.concatenate([own, jnp.broadcast_to(dec, (LANES, tq)).astype(BF16)], axis=0)

        def step(kj, ntiles, diag=False):
            kt = k_rows(kf_ref, kj, ntiles)
            vt = v_cols(kj, ntiles)
            sts = [_dot(kt, qmf_ref[hh]) for hh in range(2)]
            if diag:
                kk = lax.broadcasted_iota(jnp.int32, sts[0].shape, 0) - (ntiles - 1) * tq
                qq = lax.broadcasted_iota(jnp.int32, sts[0].shape, 1)
                sts = [jnp.where(kk <= qq, st, NEG) for st in sts]
            update(sts, [vt[hh * HEAD:(hh + 1) * HEAD, :] for hh in range(2)])

        far_tiles(step, jnp.maximum(qi - 1, 0))

        @pl.when(qi >= 1)
        def _():
            step(qi - 1, 2, True)

        @pl.when(qi == 0)
        def _():
            step(qi, 1, True)

        outs = []
        for hh in range(2):
            o = acc_ref[hh] / l_ref[hh]
            ms = jnp.mean(o * o, axis=0, keepdims=True)
            outs.append(o * lax.rsqrt(ms + EPS) * gf_ref[...])
        o_ref[...] = jnp.concatenate(outs, axis=0).T


def _attn_p(qt, kd, kf, vt, bias_p, lambda_qk, gdc, gfc, l, n_total, b, t, tq, lam_init):
    nq = t // tq
    ngrp = 2 * MIXW // LANES
    nd = MIXW // LANES
    return pl.pallas_call(
        functools.partial(_attn_p_body, tq=tq, lam_init=lam_init),
        grid=(b, ngrp, nq),
        in_specs=[
            pl.BlockSpec((None, None, LANES, tq), lambda bi, g, qi: (bi, qi, g, 0)),
            pl.BlockSpec((t, LANES), lambda bi, g, qi: (bi, jnp.minimum(g, nd - 1))),
            pl.BlockSpec((t, 2 * LANES), lambda bi, g, qi: (bi, jnp.maximum(g - nd, 0))),
            pl.BlockSpec((None, nq, LANES, tq), lambda bi, g, qi: (bi, 0, g, 0)),
            pl.BlockSpec((2, 2 * tq, tq), lambda bi, g, qi: (jnp.minimum(g, nd - 1), 0, 0)),
            pl.BlockSpec((None, 4, DQK), lambda bi, g, qi: (l, 0, 0)),
            pl.BlockSpec((None, HEAD, tq), lambda bi, g, qi: (l, 0, 0)),
            pl.BlockSpec((None, HEAD, tq), lambda bi, g, qi: (l, 0, 0)),
        ],
        out_specs=pl.BlockSpec((tq, LANES), lambda bi, g, qi: (bi * nq + qi, g)),
        out_shape=jax.ShapeDtypeStruct((n_total, 2 * MIXW), F32),
        scratch_shapes=[
            pltpu.VMEM((4, LANES, tq), BF16),
            pltpu.VMEM((2, 2 * LANES, tq), BF16),
            pltpu.VMEM((4, 1, tq), F32),
            pltpu.VMEM((4, 1, tq), F32),
            pltpu.VMEM((4, HEAD, tq), F32),
        ],
        compiler_params=_cparams(("parallel", "parallel", "arbitrary")),
        name="attn_prompt",
    )(qt, kd, kf, vt, bias_p, lambda_qk, gdc, gfc)


def _attn_s_body(*refs, npg, nsteps, dt, lam_init):
    (q_ref, ndk_ref, ndv_ref, nfk_ref, nfv_ref, lfn_ref, bl_ref, bn_ref, lq_ref, gd_ref, gf_ref) = refs[1:12]
    page_refs = refs[12:12 + 5 * npg]
    o_ref = refs[13 + 5 * npg]
    (qd_ref, qf_ref, md_ref, ld_ref, accd_ref, mf_ref, lf_ref, accf_ref, carry_ref) = refs[14 + 5 * npg:]
    j = pl.program_id(1)
    nd = 2 * NH * dt
    nf = NH * dt

    @pl.when(j == 0)
    def _init():
        q = q_ref[...]
        lane = lax.broadcasted_iota(jnp.int32, (1, MIXW), 1)
        rd = lax.broadcasted_iota(jnp.int32, (nd, 1), 0)
        qd = jnp.concatenate([q[:, :MIXW]] * (2 * NH), axis=0)
        keep = (lane // HEAD == (rd // dt) % NH) & ((lane // DQK) % 2 == rd // (NH * dt))
        qd_ref[...] = jnp.where(keep, qd, 0.0).astype(BF16)
        rf = lax.broadcasted_iota(jnp.int32, (nf, 1), 0)
        qf = jnp.concatenate([q[:, MIXW:]] * NH, axis=0)
        qf_ref[...] = jnp.where(lane // HEAD == rf // dt, qf, 0.0).astype(BF16)
        md_ref[...] = jnp.full(md_ref.shape, NEG, F32)
        ld_ref[...] = jnp.zeros_like(ld_ref)
        accd_ref[...] = jnp.zeros_like(accd_ref)
        mf_ref[...] = jnp.full(mf_ref.shape, NEG, F32)
        lf_ref[...] = jnp.zeros_like(lf_ref)
        accf_ref[...] = jnp.zeros_like(accf_ref)
        carry_ref[...] = jnp.zeros_like(carry_ref)

    def update(groups, nt):
        stats = []
        for ss, m_ref, l_ref, _, _ in groups:
            mx = ss[0]
            for s in ss[1:]:
                mx = jnp.maximum(mx, s)
            m_prev = m_ref[...]
            m_new = jnp.maximum(m_prev, jnp.max(mx, axis=-1, keepdims=True))
            alpha = jnp.exp2(m_prev - m_new)
            ps = [jnp.exp2(s - m_new) for s in ss]
            tot = ps[0]
            for p in ps[1:]:
                tot = tot + p
            l_new = alpha * l_ref[...] + jnp.sum(tot, axis=-1, keepdims=True)
            stats.append((m_new, alpha, ps, l_new))
        pvs = []
        for (_, _, _, _, vs), (_, _, ps, _) in zip(groups, stats):
            pv = None
            for p, v in zip(ps, vs):
                term = _dot_nt(p.astype(BF16), v) if nt else _dot(p.astype(BF16), v)
                pv = term if pv is None else pv + term
            pvs.append(pv)
        for (_, m_ref, l_ref, acc_ref, _), (m_new, alpha, _, l_new), pv in zip(groups, stats, pvs):
            acc_ref[...] = acc_ref[...] * alpha + pv
            m_ref[...] = m_new
            l_ref[...] = l_new

    tr = lax.broadcasted_iota(jnp.int32, (LANES, LANES), 0)
    tc = lax.broadcasted_iota(jnp.int32, (LANES, LANES), 1)
    tri = jnp.where(tr <= tc, 1.0, 0.0).astype(BF16)

    def prefix_rows(lfs):
        rows = jnp.concatenate(
            [jnp.broadcast_to(lf[hh:hh + 1, :], (dt, LANES)) for lf in lfs for hh in range(NH)], axis=0)
        hi, mid, lo = _split3(rows)
        return _dot(hi, tri) + _dot(mid, tri) + _dot(lo, tri)

    bl = jnp.concatenate([bl_ref[...]] * 2, axis=0)
    ss_d = []
    for p in range(npg):
        s = _dot(qd_ref[...], page_refs[5 * p][...].astype(BF16))
        if p == npg - 1:
            s = s + jnp.where(j == nsteps - 1, bl, 0.0)
        ss_d.append(s)

    cs = prefix_rows([page_refs[5 * p + 4][...] for p in range(npg)])
    carry = carry_ref[...]
    ss_f = []
    for p in range(npg):
        ck = cs[p * nf:(p + 1) * nf, :] + carry
        carry = ck[:, LANES - 1:LANES]
        ss_f.append(_dot(qf_ref[...], page_refs[5 * p + 2][...].astype(BF16)) - ck * LOG2E)
    carry_ref[...] = carry

    update([(ss_d, md_ref, ld_ref, accd_ref, [page_refs[5 * p + 1][...].astype(BF16) for p in range(npg)]),
            (ss_f, mf_ref, lf_ref, accf_ref, [page_refs[5 * p + 3][...].astype(BF16) for p in range(npg)])],
           True)

    @pl.when(j == nsteps - 1)
    def _final():
        def pad_rows(x):
            return jnp.concatenate([x, jnp.zeros((LANES - dt, x.shape[1]), F32)], axis=0).astype(BF16)

        sd = _dot_nt(qd_ref[...], pad_rows(ndk_ref[...]))
        sd = sd + jnp.concatenate([bn_ref[...]] * 2, axis=0)

        ckn = prefix_rows([lfn_ref[...]]) + carry_ref[...]
        rt = lax.broadcasted_iota(jnp.int32, (nf, LANES), 0) % dt
        kk = lax.broadcasted_iota(jnp.int32, (nf, LANES), 1)
        sf = _dot_nt(qf_ref[...], pad_rows(nfk_ref[...])) - ckn * LOG2E
        sf = jnp.where((kk <= rt) & (kk < dt), sf, NEG)
        update([([sd], md_ref, ld_ref, accd_ref, [pad_rows(ndv_ref[...])]),
                ([sf], mf_ref, lf_ref, accf_ref, [pad_rows(nfv_ref[...])])], False)

        head = lax.broadcasted_iota(jnp.int32, (1, MIXW), 1) // HEAD

        def block_diag(acc, l, base):
            out = jnp.zeros((dt, MIXW), F32)
            for hh in range(NH):
                r0 = base + hh * dt
                out = out + jnp.where(head == hh, acc[r0:r0 + dt, :] / l[r0:r0 + dt, :], 0.0)
            return out

        accd, ld = accd_ref[...], ld_ref[...]
        lam = _lambda(lq_ref, lam_init)
        od = block_diag(accd, ld, 0) - lam * block_diag(accd, ld, NH * dt)
        od = _head_rms(od, gd_ref[...], NH) * (1.0 - lam_init)
        of = _head_rms(block_diag(accf_ref[...], lf_ref[...], 0), gf_ref[...], NH)
        o_ref[...] = jnp.concatenate([od, of], axis=1)


def _attn_s(pt_flat, q_s, rows_s, lfn, bias_last, bias_new, lambda_qk, gd, gf, caches, mix, l,
            n_prompt, db, dt, npages, npg, lam_init):
    nsteps = npages // npg
    cdk, cdv, cfk, cfv, clf = caches
    feat, ps = cdk.shape[2], cdk.shape[3]
    in_specs = [
        pl.BlockSpec((dt, 2 * MIXW), lambda b, j, pt: (b, 0)),
    ] + [pl.BlockSpec((None, dt, MIXW), lambda b, j, pt: (l, b, 0))] * 4 + [
        pl.BlockSpec((None, NH, LANES), lambda b, j, pt: (b, 0, 0)),
        pl.BlockSpec((NH * dt, LANES), lambda b, j, pt: (0, 0)),
        pl.BlockSpec((NH * dt, LANES), lambda b, j, pt: (0, 0)),
        pl.BlockSpec((None, 4, DQK), lambda b, j, pt: (l, 0, 0)),
        pl.BlockSpec((None, 1, MIXW), lambda b, j, pt: (l, 0, 0)),
        pl.BlockSpec((None, 1, MIXW), lambda b, j, pt: (l, 0, 0)),
    ]
    args = [q_s] + list(rows_s[:4]) + [lfn, bias_last, bias_new, lambda_qk, gd, gf]
    for p in range(npg):
        def page_idx(b, j, pt, p=p):
            return (l, pt[b * npages + j * npg + p], 0, 0)
        in_specs += [pl.BlockSpec((None, None, feat, ps), page_idx)] * 4
        in_specs += [pl.BlockSpec((None, None, NH, ps), page_idx)]
        args += [cdk, cdv, cfk, cfv, clf]
    in_specs.append(pl.BlockSpec(memory_space=pl.ANY))
    args.append(mix)
    nd, nf = 2 * NH * dt, NH * dt
    return pl.pallas_call(
        functools.partial(_attn_s_body, npg=npg, nsteps=nsteps, dt=dt, lam_init=lam_init),
        grid_spec=pltpu.PrefetchScalarGridSpec(
            num_scalar_prefetch=1,
            grid=(db, nsteps),
            in_specs=in_specs,
            out_specs=pl.BlockSpec((dt, 2 * MIXW), lambda b, j, pt: (n_prompt // dt + b, 0)),
            scratch_shapes=[
                pltpu.VMEM((nd, MIXW), BF16), pltpu.VMEM((nf, MIXW), BF16),
                pltpu.VMEM((nd, 1), F32), pltpu.VMEM((nd, 1), F32), pltpu.VMEM((nd, MIXW), F32),
                pltpu.VMEM((nf, 1), F32), pltpu.VMEM((nf, 1), F32), pltpu.VMEM((nf, MIXW), F32),
                pltpu.VMEM((nf, 1), F32),
            ],
        ),
        out_shape=jax.ShapeDtypeStruct(mix.shape, mix.dtype),
        input_output_aliases={len(args): 0},
        compiler_params=_cparams(("parallel", "arbitrary")),
        name="attn_sample",
    )(pt_flat, *args)


def _largest_tile(n, candidates):
    for c in candidates:
        if all(x % c == 0 for x in n):
            return c
    raise ValueError(f"no tile in {candidates} divides {n}")


def kernel(x_prompt, x_sample, cache_diff_k, cache_diff_v, cache_fox_k, cache_fox_v, cache_fox_logf,
           page_table, norm_g, w_ffn_in, w_ffn_out, w_in, b_forget, lambda_qk, diff_norm_g, fox_norm_g,
           w_out, t5_table):
    b, t, d = x_prompt.shape
    db, dt, _ = x_sample.shape
    depth, npool, ps = cache_diff_k.shape[:3]
    npages = page_table.shape[1]
    n_p, n_s = b * t, db * dt
    n = n_p + n_s
    assert ps == LANES and dt == 8 and cache_diff_k.shape[3:] == (NH, HEAD)

    tm = _largest_tile((n_p, n_s), (512, 256, 128))
    tq = _largest_tile((t,), (256, 128))
    tm_p = _largest_tile((t,), (512, 256, 128))
    tm_s = min(tm, n_s)
    page_bytes = 4 * NH * HEAD * ps * 4
    npg = _largest_tile((npages,), [c for c in (16, 8, 4, 2, 1) if 2 * c * page_bytes <= VMEM_LIMIT * 5 // 8])

    wi = w_ffn_in.astype(BF16)
    wo = w_ffn_out.astype(BF16)
    wout = w_out.astype(BF16)
    o_dk, o_dv, o_fq, o_fk, o_fv, o_ff = (MIXW * k for k in range(1, 7))
    wq = jnp.concatenate([w_in[:, :, :o_dk], w_in[:, :, o_fq:o_fk]], axis=2).astype(BF16)
    wkv = jnp.concatenate([w_in[:, :, o_dk:o_fq], w_in[:, :, o_fk:o_ff]], axis=2).astype(BF16)
    wk = jnp.concatenate([w_in[:, :, o_dk:o_dv], w_in[:, :, o_fk:o_fv]], axis=2).astype(BF16)
    wqt = jnp.swapaxes(wq, 1, 2)
    wkvt = jnp.swapaxes(wkv, 1, 2)
    wf = jnp.pad(w_in[:, :, o_ff:], ((0, 0), (0, 0), (0, LANES - NH))).astype(BF16)
    wft = jnp.swapaxes(wf[:, :, :16], 1, 2)
    bfrow = jnp.pad(b_forget, ((0, 0), (0, LANES - NH)))[:, None, :]
    bfcol = jnp.pad(b_forget, ((0, 0), (0, 16 - NH)))[:, :, None]
    gd = jnp.tile(diff_norm_g, (1, NH))[:, None, :]
    gf = jnp.tile(fox_norm_g, (1, NH))[:, None, :]
    gdc = jnp.broadcast_to(diff_norm_g[:, :, None], (depth, HEAD, tq))
    gfc = jnp.broadcast_to(fox_norm_g[:, :, None], (depth, HEAD, tq))
    pt_flat = page_table.reshape(-1)

    def page_major(c):
        return jnp.transpose(c, (0, 1, 3, 4, 2)).reshape(depth, npool, NH * HEAD, ps)
    caches = (page_major(cache_diff_k), page_major(cache_diff_v), page_major(cache_fox_k),
              page_major(cache_fox_v), jnp.transpose(cache_fox_logf, (0, 1, 3, 2)))

    bias_p, bias_last, bias_new = _bias_tiles(t5_table, tq, dt)
    bias_last = bias_last.reshape(NH * dt, LANES)
    bias_new = bias_new.reshape(NH * dt, LANES)

    h = jnp.concatenate([x_prompt.reshape(n_p, d), x_sample.reshape(n_s, d)], axis=0)
    rows_p = rows_s = None
    for l in range(depth):
        lam_init = 0.8 - 0.6 * math.exp(-0.3 * l)
        h = _ffn_a(h, norm_g, wi, wo, l, tm)
        outs = _proj_p(h, rows_p, norm_g, wqt, wkvt, wk, wft, bfcol, wf, bfrow, l, depth, b, t, tm_p, tq)
        rows_p, (qt, kd, kf, vt) = outs[:5], outs[5:]
        outs = _proj_s(h, n_p // tm_s, rows_s, norm_g, wq, wkv, wf, bfrow, l, depth, n_s, tm_s)
        rows_s, q_s = outs[:5], outs[5]
        lfn = jnp.transpose(rows_s[4][l].reshape(db, dt, NH), (0, 2, 1))
        lfn = jnp.pad(lfn, ((0, 0), (0, 0), (0, LANES - dt)))
        mix = _attn_p(qt, kd, kf, vt, bias_p, lambda_qk, gdc, gfc, l, n, b, t, tq, lam_init)
        mix = _attn_s(pt_flat, q_s, rows_s, lfn, bias_last, bias_new, lambda_qk, gd, gf, caches, mix, l,
                      n_p, db, dt, npages, npg, lam_init)
        h = _ffn_b(h, mix, norm_g, wout, wi, wo, l, tm)

    def rows_out_p(r):
        return jnp.transpose(r.reshape(depth, b, NH, HEAD, t), (0, 1, 4, 2, 3))

    def rows_out_s(r):
        return r.reshape(depth, db, dt, NH, HEAD)

    return (h[:n_p].reshape(b, t, d), h[n_p:].reshape(db, dt, d),
            rows_out_p(rows_p[0]), rows_out_p(rows_p[1]), rows_out_p(rows_p[2]), rows_out_p(rows_p[3]),
            jnp.transpose(rows_p[4], (0, 1, 3, 2)),
            rows_out_s(rows_s[0]), rows_out_s(rows_s[1]), rows_out_s(rows_s[2]), rows_out_s(rows_s[3]),
            rows_s[4].reshape(depth, db, dt, NH))
```
